```python
import jax, jax.numpy as jnp
from jax import lax
import numpy as np

D_MODEL = 1024
BATCH = 2
SEQ = 8192
DEPTH = 4
DEC_BATCH = 128
DEC_SEQ = 1
PAST_LEN = 8192
PAGE_SIZE = 128

BRANCH_WIDTH = 256
N_BRANCH = 4
GMLP_GROUPS = 4
GMLP_CHUNK = 128
CONV_WIDTH = 31
MLA_HEADS = 4
MLA_Q_RANK = 256
MLA_KV_RANK = 128
MLA_NOPE = 64
MLA_ROPE = 32
MLA_V = 64
MLA_SCALE = (MLA_NOPE + MLA_ROPE) ** -0.5
ROPE_THETA = 10000.0
ATTN_QBLOCK = 128
HGRN_HEADS = 4
HGRN_DK = 64
HGRN_DV = 64
HGRN_CHUNK = 64
MLP_HIDDEN = 4 * D_MODEL
ADA_PARTS = 6
EPS = 1e-6
MASK_VALUE = -1e30
F_FLOOR = 1e-30

IN_SIZES = (BRANCH_WIDTH, BRANCH_WIDTH,
            BRANCH_WIDTH, BRANCH_WIDTH,
            MLA_Q_RANK, MLA_KV_RANK, MLA_ROPE,
            BRANCH_WIDTH, BRANCH_WIDTH, BRANCH_WIDTH, BRANCH_WIDTH,
            N_BRANCH * D_MODEL)
IN_COLS = 8 * BRANCH_WIDTH + MLA_Q_RANK + MLA_KV_RANK + MLA_ROPE + N_BRANCH * D_MODEL

kernel_name = "hybrid_gmlp_conv_mla_hgrn2_decoder_step"


def rmsnorm(x, g):
    x32 = x.astype(jnp.float32)
    y = x32 * lax.rsqrt(jnp.mean(x32 * x32, axis=-1, keepdims=True) + EPS)
    return (y * g.astype(jnp.float32)).astype(x.dtype)


def layernorm(x, g, b):
    x32 = x.astype(jnp.float32)
    xc = x32 - jnp.mean(x32, axis=-1, keepdims=True)
    y = xc * lax.rsqrt(jnp.mean(xc * xc, axis=-1, keepdims=True) + EPS)
    return (y * g.astype(jnp.float32) + b.astype(jnp.float32)).astype(x.dtype)


def rope(x, pos):
    p = x.shape[-1]
    half = p // 2
    inv = ROPE_THETA ** (-(2.0 / p) * jnp.arange(half, dtype=jnp.float32))
    ang = pos[:, None] * inv[None, :]
    cos = jnp.cos(ang)[None, :, None, :].astype(x.dtype)
    sin = jnp.sin(ang)[None, :, None, :].astype(x.dtype)
    x1, x2 = x[..., :half], x[..., half:]
    return jnp.concatenate([x1 * cos - x2 * sin, x2 * cos + x1 * sin], axis=-1)


def split_cols(z):
    outs, start = [], 0
    for size in IN_SIZES:
        outs.append(z[..., start:start + size])
        start += size
    return outs


def gmlp_branch(u, v, ln_g, ln_b, ws, bs):
    B, T, C = v.shape
    L = min(GMLP_CHUNK, T)
    n = T // L
    u = jax.nn.gelu(u)
    vn = layernorm(jax.nn.gelu(v), ln_g, ln_b)
    vc = vn.reshape(B, n, L, GMLP_GROUPS, C // GMLP_GROUPS)
    w = ws[:, :L, :L] * jnp.tril(jnp.ones((L, L), ws.dtype))[None]
    mixed = jnp.einsum('gts,bnsgc->bntgc', w, vc) + jnp.swapaxes(bs[:, :L], 0, 1)[None, None, :, :, None]
    return u * mixed.reshape(B, T, C), vn


def conv_branch(a, b, prefix, w, bias, ln_g, ln_b):
    g = a * jax.nn.sigmoid(b)
    full = jnp.concatenate([prefix.astype(g.dtype), g], axis=1)
    C = g.shape[-1]
    y = lax.conv_general_dilated(full, w[:, None, :].astype(g.dtype), window_strides=(1,), padding='VALID',
                                 dimension_numbers=('NWC', 'WIO', 'NWC'), feature_group_count=C) + bias
    y = jax.nn.silu(layernorm(y, ln_g, ln_b))
    return y, full[:, -(CONV_WIDTH - 1):]


def mla_project(cq, ckv, kr, pos, q_norm_g, w_uq, kv_norm_g, w_uk):
    q = jnp.einsum('btr,rhd->bthd', rmsnorm(cq, q_norm_g), w_uq)
    q_pe = rope(q[..., MLA_NOPE:], pos)
    q_lat = jnp.einsum('bthd,rhd->bthr', q[..., :MLA_NOPE], w_uk)
    lat = rmsnorm(ckv, kv_norm_g)
    k_pe = rope(kr[:, :, None, :], pos)[:, :, 0]
    return q_lat, q_pe, lat, k_pe


def mla_prompt_attend(q_lat, q_pe, lat, k_pe):
    B, T, H, R = q_lat.shape
    qb = min(ATTN_QBLOCK, T)
    nb = T // qb

    def blocks(a):
        return jnp.moveaxis(a.reshape((B, nb, qb) + a.shape[2:]), 1, 0)

    qpos = jnp.arange(T).reshape(nb, qb)
    kpos = jnp.arange(T)

    def one(inp):
        ql, qp, qi = inp
        s = (jnp.einsum('bqhr,bkr->bhqk', ql, lat) + jnp.einsum('bqhp,bkp->bhqk', qp, k_pe)).astype(jnp.float32)
        s = jnp.where(kpos[None, :] <= qi[:, None], s * MLA_SCALE, MASK_VALUE)
        p = jax.nn.softmax(s, axis=-1).astype(lat.dtype)
        return jnp.einsum('bhqk,bkr->bqhr', p, lat)

    o = lax.map(one, (blocks(q_lat), blocks(q_pe), qpos))
    return jnp.moveaxis(o, 0, 1).reshape(B, T, H, R)


def mla_sample_attend(q_lat, q_pe, lat, k_pe, lat_past, kpe_past):
    T = q_lat.shape[1]
    P = lat_past.shape[1]
    s_past = (jnp.einsum('bqhr,bkr->bhqk', q_lat, lat_past) + jnp.einsum('bqhp,bkp->bhqk', q_pe, kpe_past)).astype(jnp.float32)
    s_new = (jnp.einsum('bqhr,bkr->bhqk', q_lat, lat) + jnp.einsum('bqhp,bkp->bhqk', q_pe, k_pe)).astype(jnp.float32)
    s_new = jnp.where(jnp.tril(jnp.ones((T, T), bool)), s_new * MLA_SCALE, MASK_VALUE)
    p = jax.nn.softmax(jnp.concatenate([s_past * MLA_SCALE, s_new], axis=-1), axis=-1).astype(lat.dtype)
    return jnp.einsum('bhqk,bkr->bqhr', p[..., :P], lat_past) + jnp.einsum('bhqk,bkr->bqhr', p[..., P:], lat)


def gated_linear_scan(q, k, v, logf, s0, chunk):
    B, T, H, K = q.shape
    V = v.shape[-1]
    n = T // chunk

    def to_chunks(a):
        return jnp.moveaxis(a.reshape(B, n, chunk, H, a.shape[-1]), 1, 0)

    tri = jnp.tril(jnp.ones((chunk, chunk), bool))[None, :, :, None, None]

    def step(s, inp):
        qc, kc, vc, gc = [a.astype(jnp.float32) for a in inp]
        b = jnp.cumsum(gc, axis=1)
        diff = jnp.where(tri, b[:, :, None] - b[:, None, :], 0.0)
        decay = jnp.where(tri, jnp.exp(diff), 0.0)
        att = jnp.einsum('bthk,bshk,btshk->bhts', qc, kc, decay)
        o = jnp.einsum('bhts,bshv->bthv', att, vc) + jnp.einsum('bthk,bhkv->bthv', qc * jnp.exp(b), s)
        b_last = b[:, -1]
        s = jnp.exp(b_last)[..., None] * s + jnp.einsum('bshk,bshv->bhkv', kc * jnp.exp(b_last[:, None] - b), vc)
        return s, o

    s, o = lax.scan(step, s0.astype(jnp.float32), (to_chunks(q), to_chunks(k), to_chunks(v), to_chunks(logf)))
    o = jnp.moveaxis(o, 0, 1).reshape(B, T, H, V).astype(q.dtype)
    return o, s.astype(s0.dtype)


def hgrn2_branch(hq, hf, hi, hg, lower, out_g, s0, chunk):
    B, T, _ = hq.shape
    shp = (B, T, HGRN_HEADS, HGRN_DK)
    q = jax.nn.silu(hq).reshape(shp)
    hf32 = hf.astype(jnp.float32)
    f = lower + (1.0 - lower) * jax.nn.sigmoid(hf32)
    logf = jnp.log(jnp.maximum(f, F_FLOOR)).reshape(shp)
    k = ((1.0 - lower) * jax.nn.sigmoid(-hf32)).astype(hq.dtype).reshape(shp)
    v = hi.reshape(B, T, HGRN_HEADS, HGRN_DV)
    o, s = gated_linear_scan(q, k, v, logf, s0, chunk)
    o = rmsnorm(o, out_g.reshape(HGRN_HEADS, HGRN_DV)) * jax.nn.silu(hg).reshape(B, T, HGRN_HEADS, HGRN_DV)
    return o.reshape(B, T, HGRN_HEADS * HGRN_DV), s


def trunk_layer(x, c, lw, lower, pos, conv_prefix, hgrn_s0, hgrn_chunk, attend):
    B, T, D = x.shape
    mod = jnp.einsum('bd,de->be', jax.nn.silu(c), lw['w_ada']) + lw['b_ada']
    sh1, sc1, gt1, sh2, sc2, gt2 = [m[:, None, :] for m in jnp.split(mod, ADA_PARTS, axis=-1)]
    h = rmsnorm(x, lw['norm1_g']) * (1 + sc1) + sh1
    z = jnp.einsum('btd,de->bte', h, lw['w_in'])
    gu, gv, ca, cb, cq, ckv, kr, hq, hf, hi, hg, gl = split_cols(z)
    ya, v_rows = gmlp_branch(gu, gv, lw['gmlp_ln_g'], lw['gmlp_ln_b'], lw['gmlp_ws'], lw['gmlp_bs'])
    yb, conv_rows = conv_branch(ca, cb, conv_prefix, lw['conv_w'], lw['conv_b'], lw['conv_ln_g'], lw['conv_ln_b'])
    q_lat, q_pe, lat, k_pe = mla_project(cq, ckv, kr, pos, lw['mla_q_norm_g'], lw['mla_w_uq'],
                                         lw['mla_kv_norm_g'], lw['mla_w_uk'])
    o_lat = attend(q_lat, q_pe, lat, k_pe)
    yc = jnp.einsum('bthr,rhv->bthv', o_lat, lw['mla_w_uv']).reshape(B, T, MLA_HEADS * MLA_V)
    yd, s_new = hgrn2_branch(hq, hf, hi, hg, lower, lw['hgrn_out_norm_g'], hgrn_s0, hgrn_chunk)
    branches = jnp.stack([ya, yb, yc, yd], axis=2)
    proj = jnp.einsum('btnw,nwd->btnd', branches, lw['w_branch'])
    gates = jax.nn.sigmoid(gl.reshape(B, T, N_BRANCH, D))
    mixed = jnp.einsum('btd,de->bte', jnp.sum(gates * proj, axis=2), lw['w_out'])
    x = x + gt1 * mixed
    h = rmsnorm(x, lw['norm2_g']) * (1 + sc2) + sh2
    u = jnp.square(jax.nn.relu(jnp.einsum('btd,df->btf', h, lw['mlp_w_up'])))
    x = x + gt2 * jnp.einsum('btf,fd->btd', u, lw['mlp_w_down'])
    return x, v_rows, conv_rows, lat, k_pe, s_new


def setup_inputs(seed: int = 0) -> dict:
    key = jax.random.key(seed)
    ks = iter(jax.random.split(key, 48))
    f32 = jnp.float32

    def nrm(shape, scale):
        return jax.random.normal(next(ks), shape, f32) * scale

    def gain(shape):
        return 1.0 + nrm(shape, 0.01)

    n_pages = PAST_LEN // PAGE_SIZE
    n_used = DEC_BATCH * n_pages
    n_pool = n_used + max(n_used // 4, 1)
    W = BRANCH_WIDTH
    return {
        "x_prompt": nrm((BATCH, SEQ, D_MODEL), 1.0),
        "x_sample": nrm((DEC_BATCH, DEC_SEQ, D_MODEL), 1.0),
        "cache_kv_latent": nrm((DEPTH, n_pool, PAGE_SIZE, MLA_KV_RANK), 1.0),
        "cache_k_rope": nrm((DEPTH, n_pool, PAGE_SIZE, MLA_ROPE), 1.0),
        "state_conv": nrm((DEPTH, DEC_BATCH, CONV_WIDTH - 1, W), 0.5),
        "state_hgrn": nrm((DEPTH, DEC_BATCH, HGRN_HEADS, HGRN_DK, HGRN_DV), 0.5),
        "page_table": jax.random.permutation(next(ks), n_pool)[:n_used].reshape(DEC_BATCH, n_pages).astype(jnp.int32),
        "c_prompt": nrm((BATCH, D_MODEL), 1.0),
        "c_sample": nrm((DEC_BATCH, D_MODEL), 1.0),
        "norm1_g": gain((DEPTH, D_MODEL)),
        "norm2_g": gain((DEPTH, D_MODEL)),
        "final_norm_g": gain((D_MODEL,)),
        "w_ada": nrm((DEPTH, D_MODEL, ADA_PARTS * D_MODEL), 0.5 * D_MODEL ** -0.5),
        "b_ada": nrm((DEPTH, ADA_PARTS * D_MODEL), 0.01),
        "w_in": nrm((DEPTH, D_MODEL, IN_COLS), D_MODEL ** -0.5),
        "gmlp_ln_g": gain((DEPTH, W)),
        "gmlp_ln_b": nrm((DEPTH, W), 0.01),
        "gmlp_ws": nrm((DEPTH, GMLP_GROUPS, GMLP_CHUNK, GMLP_CHUNK), GMLP_CHUNK ** -0.5),
        "gmlp_bs": gain((DEPTH, GMLP_GROUPS, GMLP_CHUNK)),
        "conv_w": nrm((DEPTH, CONV_WIDTH, W), CONV_WIDTH ** -0.5),
        "conv_b": nrm((DEPTH, W), 0.01),
        "conv_ln_g": gain((DEPTH, W)),
        "conv_ln_b": nrm((DEPTH, W), 0.01),
        "mla_q_norm_g": gain((DEPTH, MLA_Q_RANK)),
        "mla_w_uq": nrm((DEPTH, MLA_Q_RANK, MLA_HEADS, MLA_NOPE + MLA_ROPE), MLA_Q_RANK ** -0.5),
        "mla_kv_norm_g": gain((DEPTH, MLA_KV_RANK)),
        "mla_w_uk": nrm((DEPTH, MLA_KV_RANK, MLA_HEADS, MLA_NOPE), MLA_KV_RANK ** -0.5),
        "mla_w_uv": nrm((DEPTH, MLA_KV_RANK, MLA_HEADS, MLA_V), MLA_KV_RANK ** -0.5),
        "hgrn_lb_logits": nrm((DEPTH, W), 0.5),
        "hgrn_out_norm_g": gain((DEPTH, HGRN_HEADS * HGRN_DV)),
        "w_branch": nrm((DEPTH, N_BRANCH, W, D_MODEL), W ** -0.5),
        "w_out": nrm((DEPTH, D_MODEL, D_MODEL), D_MODEL ** -0.5),
        "mlp_w_up": nrm((DEPTH, D_MODEL, MLP_HIDDEN), D_MODEL ** -0.5),
        "mlp_w_down": nrm((DEPTH, MLP_HIDDEN, D_MODEL), MLP_HIDDEN ** -0.5),
    }


def reference(x_prompt, x_sample, cache_kv_latent, cache_k_rope, state_conv, state_hgrn, page_table,
              c_prompt, c_sample, norm1_g, norm2_g, final_norm_g, w_ada, b_ada, w_in,
              gmlp_ln_g, gmlp_ln_b, gmlp_ws, gmlp_bs, conv_w, conv_b, conv_ln_g, conv_ln_b,
              mla_q_norm_g, mla_w_uq, mla_kv_norm_g, mla_w_uk, mla_w_uv,
              hgrn_lb_logits, hgrn_out_norm_g, w_branch, w_out, mlp_w_up, mlp_w_down):
    bp, tp, _ = x_prompt.shape
    bs_, ts, _ = x_sample.shape
    past_len = page_table.shape[1] * cache_kv_latent.shape[2]
    pos_p = jnp.arange(tp, dtype=jnp.float32)
    pos_s = past_len + jnp.arange(ts, dtype=jnp.float32)
    sm = jax.nn.softmax(hgrn_lb_logits.astype(jnp.float32), axis=0)
    lower = jnp.cumsum(sm, axis=0) - sm[0]

    xp, xs = x_prompt, x_sample
    lat_p, kpe_p, lat_s, kpe_s, conv_p, conv_s, hg_p, hg_s, v_s = [], [], [], [], [], [], [], [], []
    for l in range(DEPTH):
        lw = dict(norm1_g=norm1_g[l], norm2_g=norm2_g[l], w_ada=w_ada[l], b_ada=b_ada[l], w_in=w_in[l],
                  gmlp_ln_g=gmlp_ln_g[l], gmlp_ln_b=gmlp_ln_b[l], gmlp_ws=gmlp_ws[l], gmlp_bs=gmlp_bs[l],
                  conv_w=conv_w[l], conv_b=conv_b[l], conv_ln_g=conv_ln_g[l], conv_ln_b=conv_ln_b[l],
                  mla_q_norm_g=mla_q_norm_g[l], mla_w_uq=mla_w_uq[l], mla_kv_norm_g=mla_kv_norm_g[l],
                  mla_w_uk=mla_w_uk[l], mla_w_uv=mla_w_uv[l], hgrn_out_norm_g=hgrn_out_norm_g[l],
                  w_branch=w_branch[l], w_out=w_out[l], mlp_w_up=mlp_w_up[l], mlp_w_down=mlp_w_down[l])
        xp, _, cp, lp, kp, sp = trunk_layer(
            xp, c_prompt, lw, lower[l], pos_p,
            jnp.zeros((bp, CONV_WIDTH - 1, BRANCH_WIDTH), xp.dtype),
            jnp.zeros((bp, HGRN_HEADS, HGRN_DK, HGRN_DV), xp.dtype),
            min(HGRN_CHUNK, tp), mla_prompt_attend)
        lat_past = cache_kv_latent[l][page_table].reshape(bs_, past_len, MLA_KV_RANK)
        kpe_past = cache_k_rope[l][page_table].reshape(bs_, past_len, MLA_ROPE)

        def attend_s(ql, qp, lt, kpn, lat_past=lat_past, kpe_past=kpe_past):
            return mla_sample_attend(ql, qp, lt, kpn, lat_past, kpe_past)

        xs, vr, cs, ls, ksn, ss = trunk_layer(
            xs, c_sample, lw, lower[l], pos_s, state_conv[l], state_hgrn[l], ts, attend_s)
        lat_p.append(lp); kpe_p.append(kp); conv_p.append(cp); hg_p.append(sp)
        lat_s.append(ls); kpe_s.append(ksn); conv_s.append(cs); hg_s.append(ss); v_s.append(vr)

    y_prompt = rmsnorm(xp, final_norm_g)
    y_sample = rmsnorm(xs, final_norm_g)
    return (y_prompt, y_sample,
            jnp.stack(lat_p), jnp.stack(kpe_p), jnp.stack(lat_s), jnp.stack(kpe_s),
            jnp.stack(conv_p), jnp.stack(conv_s), jnp.stack(hg_p), jnp.stack(hg_s), jnp.stack(v_s))
```

```python
import functools

import jax
import jax.numpy as jnp
from jax import lax
from jax.experimental import pallas as pl
from jax.experimental.pallas import tpu as pltpu

F32 = jnp.float32
BF16 = jnp.bfloat16

D_MODEL = 1024
BRANCH_WIDTH = 256
N_BRANCH = 4
GMLP_GROUPS = 4
GMLP_CHUNK = 128
CONV_WIDTH = 31
MLA_HEADS = 4
MLA_Q_RANK = 256
MLA_KV_RANK = 128
MLA_NOPE = 64
MLA_ROPE = 32
MLA_V = 64
MLA_SCALE = (MLA_NOPE + MLA_ROPE) ** -0.5
ROPE_THETA = 10000.0
HGRN_HEADS = 4
HGRN_DK = 64
HGRN_DV = 64
MLP_HIDDEN = 4 * D_MODEL
ADA_PARTS = 6
EPS = 1e-6
MASK_VALUE = -1e30
F_FLOOR = 1e-30

OFF_KR = 8 * 0 + 4 * BRANCH_WIDTH + MLA_Q_RANK + MLA_KV_RANK
OFF_H = OFF_KR + MLA_ROPE
OFF_HF = OFF_H + BRANCH_WIDTH
OFF_GATE = OFF_H + 4 * BRANCH_WIDTH

LANES = 128
ROW_TILE = 256
ATTN_TILE = 256
HGRN_SUB = 16
SAMPLE_TILE = 32
CONV_PAD = 32
VMEM_LIMIT = 56 * 1024 * 1024


def _dot(a, b):
    return jnp.dot(a.astype(BF16), b.astype(BF16), preferred_element_type=F32)


def _dot_nt(a, b):
    return lax.dot_general(a.astype(BF16), b.astype(BF16), (((1,), (1,)), ((), ())),
                           preferred_element_type=F32)


def _split3(x):
    hi = x.astype(BF16)
    r = x - hi.astype(F32)
    mid = r.astype(BF16)
    lo = (r - mid.astype(F32)).astype(BF16)
    return hi, mid, lo


def _dot_sel_rhs(x, c):
    hi, mid, lo = _split3(x)
    return (jnp.dot(lo, c, preferred_element_type=F32) + jnp.dot(mid, c, preferred_element_type=F32)
            + jnp.dot(hi, c, preferred_element_type=F32))


def _dot_sel_lhs(c, x):
    hi, mid, lo = _split3(x)
    return (jnp.dot(c, lo, preferred_element_type=F32) + jnp.dot(c, mid, preferred_element_type=F32)
            + jnp.dot(c, hi, preferred_element_type=F32))


def _rms(x, g):
    return x * lax.rsqrt(jnp.mean(x * x, axis=-1, keepdims=True) + EPS) * g


def _layernorm(x, g, b):
    xc = x - jnp.mean(x, axis=-1, keepdims=True)
    return xc * lax.rsqrt(jnp.mean(xc * xc, axis=-1, keepdims=True) + EPS) * g + b


def _sigmoid(x):
    return jax.nn.sigmoid(x)


def _silu(x):
    return x * jax.nn.sigmoid(x)


def _gelu(x):
    return jax.nn.gelu(x)


def _iota(shape, dim):
    return lax.broadcasted_iota(jnp.int32, shape, dim)


def _head_ones():
    r = _iota((BRANCH_WIDTH, BRANCH_WIDTH), 0) >> 6
    c = _iota((BRANCH_WIDTH, BRANCH_WIDTH), 1) >> 6
    return r == c


def _rope_rotate(x, cos_t, sin_a, sin_b):
    return (x * cos_t + pltpu.roll(x, LANES - MLA_ROPE // 2, 1) * sin_a
            + pltpu.roll(x, MLA_ROPE // 2, 1) * sin_b)


def _mla_project(za, hb, wkr_ref, qng_ref, wuqn_ref, wuqr_ref, wukbd_ref, kvng_ref, cos_t, sin_a, sin_b):
    cq = za[:, 4 * BRANCH_WIDTH:4 * BRANCH_WIDTH + MLA_Q_RANK]
    ckv = za[:, 4 * BRANCH_WIDTH + MLA_Q_RANK:OFF_KR]
    cqn = _rms(cq, qng_ref[...]).astype(BF16)
    qn = jnp.dot(cqn, wuqn_ref[...], preferred_element_type=F32)
    qr = jnp.dot(cqn, wuqr_ref[...], preferred_element_type=F32)
    qlat = _dot(qn, wukbd_ref[...])
    lat = _rms(ckv, kvng_ref[...])
    kr = jnp.dot(hb, wkr_ref[...], preferred_element_type=F32)
    kpe = _rope_rotate(kr, cos_t, sin_a, sin_b)
    heads = []
    for h in range(MLA_HEADS):
        ql = qlat[:, h * LANES:(h + 1) * LANES]
        qp = _rope_rotate(qr[:, h * LANES:(h + 1) * LANES], cos_t, sin_a, sin_b)
        heads.append((ql, qp))
    return heads, lat, kpe


def _ada_kernel(c_ref, w_ref, b_ref, o_ref):
    c = c_ref[...]
    a = _silu(c).astype(BF16)
    o_ref[0] = jnp.dot(a, w_ref[0].astype(BF16), preferred_element_type=F32) + b_ref[0]


def _ada(c_all, w_ada, b_ada):
    depth = w_ada.shape[0]
    mb = c_all.shape[0]
    n = ADA_PARTS * D_MODEL
    tn = 1536
    return pl.pallas_call(
        _ada_kernel,
        grid=(depth, n // tn),
        in_specs=[pl.BlockSpec((mb, D_MODEL), lambda l, j: (0, 0)),
                  pl.BlockSpec((1, D_MODEL, tn), lambda l, j: (l, 0, j)),
                  pl.BlockSpec((1, 1, tn), lambda l, j: (l, 0, j))],
        out_specs=pl.BlockSpec((1, mb, tn), lambda l, j: (l, 0, j)),
        out_shape=jax.ShapeDtypeStruct((depth, mb, n), F32),
        compiler_params=pltpu.CompilerParams(dimension_semantics=("arbitrary", "arbitrary"),
                                             vmem_limit_bytes=VMEM_LIMIT),
        name="ada_mod",
    )(c_all, w_ada, b_ada.reshape(depth, 1, n))


def _p1_kernel(x_ref, sh1_ref, sc1_ref, n1g_ref,
               wa_ref, wkr_ref, wh_ref, whft_ref, wg_ref, wbr_ref,
               gws_ref, gbias_ref, glng_ref, glnb_ref,
               cw_ref, cb_ref, clng_ref, clnb_ref,
               qng_ref, wuqn_ref, wuqr_ref, wukbd_ref, kvng_ref,
               cos_ref, sina_ref, sinb_ref,
               low_ref, lowt_ref, hog_ref,
               q_out, k_out, lat_out, kpe_out, mp_out, gc_out, conv_out, hs_out,
               gbuf, s_ref, o_scr):
    tm = x_ref.shape[1]
    i = pl.program_id(1)

    @pl.when(i == 0)
    def _():
        gbuf[0:CONV_PAD, :] = jnp.zeros((CONV_PAD, BRANCH_WIDTH), F32)
        s_ref[...] = jnp.zeros(s_ref.shape, F32)

    x = x_ref[0]
    h = _rms(x, n1g_ref[...]) * (1.0 + sc1_ref[...]) + sh1_ref[...]
    hb = h.astype(BF16)

    za = jnp.dot(hb, wa_ref[...], preferred_element_type=F32)
    bw = BRANCH_WIDTH

    u = _gelu(za[:, 0:bw])
    vn = _layernorm(_gelu(za[:, bw:2 * bw]), glng_ref[...], glnb_ref[...])
    tri = _iota((GMLP_CHUNK, GMLP_CHUNK), 1) <= _iota((GMLP_CHUNK, GMLP_CHUNK), 0)
    wcat = jnp.concatenate(
        [jnp.where(tri, gws_ref[g], 0.0).astype(BF16) for g in range(GMLP_GROUPS)], axis=1)
    grp = _iota((GMLP_CHUNK, bw), 1) >> 6
    mixed = []
    for c in range(tm // GMLP_CHUNK):
        vc = vn[c * GMLP_CHUNK:(c + 1) * GMLP_CHUNK]
        bd = jnp.concatenate([jnp.where(grp == g, vc, 0.0).astype(BF16) for g in range(GMLP_GROUPS)], axis=0)
        mixed.append(jnp.dot(wcat, bd, preferred_element_type=F32) + gbias_ref[...])
    ya = u * jnp.concatenate(mixed, axis=0)

    g = za[:, 2 * bw:3 * bw] * _sigmoid(za[:, 3 * bw:4 * bw])
    gbuf[CONV_PAD:CONV_PAD + tm, :] = g
    acc = jnp.zeros((tm, bw), F32)
    for j in range(CONV_WIDTH):
        acc = acc + cw_ref[j:j + 1, :] * gbuf[pl.ds(CONV_PAD - (CONV_WIDTH - 1) + j, tm), :]
    yb = _silu(_layernorm(acc + cb_ref[...], clng_ref[...], clnb_ref[...]))
    tail = g[tm - CONV_PAD:tm]
    gbuf[0:CONV_PAD, :] = tail
    conv_out[0] = tail

    heads, lat, kpe = _mla_project(za, hb, wkr_ref, qng_ref, wuqn_ref, wuqr_ref, wukbd_ref, kvng_ref,
                                   cos_ref[...], sina_ref[...], sinb_ref[...])
    for hh, (ql, qp) in enumerate(heads):
        q_out[0, hh, :, 0:LANES] = ql.astype(BF16)
        q_out[0, hh, :, LANES:2 * LANES] = qp.astype(BF16)
    k_out[0, :, 0:LANES] = lat.astype(BF16)
    k_out[0, :, LANES:2 * LANES] = kpe.astype(BF16)
    lat_out[0] = lat
    kpe_out[0] = kpe[:, 0:MLA_ROPE]

    zh = jnp.dot(hb, wh_ref[...], preferred_element_type=F32)
    low = low_ref[...]
    hf = zh[:, bw:2 * bw]
    q = _silu(zh[:, 0:bw])
    logf = jnp.log(jnp.maximum(low + (1.0 - low) * _sigmoid(hf), F_FLOOR))
    kk = (1.0 - low) * _sigmoid(-hf)
    v = zh[:, 2 * bw:3 * bw]
    og = _silu(zh[:, 3 * bw:4 * bw])
    hft = _dot_nt(whft_ref[...], hb)
    lowt = lowt_ref[...]
    logft = jnp.log(jnp.maximum(lowt + (1.0 - lowt) * _sigmoid(hft), F_FLOOR))
    kt = (1.0 - lowt) * _sigmoid(-hft)

    rr = _iota((tm, tm), 0)
    cc = _iota((tm, tm), 1)
    same = (rr >> 4) == (cc >> 4)
    m_incl = jnp.where(same & (cc <= rr), 1.0, 0.0).astype(BF16)
    m_after = jnp.where(same & (rr > cc), 1.0, 0.0).astype(BF16)
    m_all = jnp.where(same, 1.0, 0.0).astype(BF16)
    b_loc = _dot_sel_lhs(m_incl, logf)
    qd = (q * jnp.exp(b_loc)).astype(BF16)
    kdt = kt * jnp.exp(_dot_sel_rhs(logft, m_after))
    att = jnp.exp(_dot_sel_rhs(logft, m_all))

    ones_h = jnp.where(_head_ones(), 1.0, 0.0).astype(BF16)
    headmask = _head_ones()
    tloc = _iota((HGRN_SUB, bw), 0)
    lane128 = _iota((bw, LANES), 1)
    for sb in range(tm // HGRN_SUB):
        r0 = sb * HGRN_SUB
        q_i = q[r0:r0 + HGRN_SUB]
        b_i = b_loc[r0:r0 + HGRN_SUB]
        k_i = kk[r0:r0 + HGRN_SUB]
        v_i = v[r0:r0 + HGRN_SUB]
        ps = []
        for s in range(HGRN_SUB):
            e = jnp.exp(jnp.minimum(b_i - b_i[s:s + 1], 0.0))
            ps.append(jnp.where(tloc >= s, q_i * k_i[s:s + 1] * e, 0.0).astype(BF16))
        rsum = jnp.dot(jnp.concatenate(ps, axis=0), ones_h, preferred_element_type=F32)
        od = jnp.zeros((HGRN_SUB, bw), F32)
        for s in range(HGRN_SUB):
            od = od + rsum[s * HGRN_SUB:(s + 1) * HGRN_SUB] * v_i[s:s + 1]
        s_prev = s_ref[...]
        o_scr[r0:r0 + HGRN_SUB, :] = od + jnp.dot(qd[r0:r0 + HGRN_SUB], s_prev.astype(BF16),
                                                  preferred_element_type=F32)
        c0 = (r0 // LANES) * LANES
        kblk = kdt[:, c0:c0 + LANES]
        kmask = jnp.where((lane128 >= r0 - c0) & (lane128 < r0 - c0 + HGRN_SUB), kblk, 0.0).astype(BF16)
        upd = jnp.dot(kmask, v[c0:c0 + LANES].astype(BF16), preferred_element_type=F32)
        a_col = att[:, r0:r0 + 1]
        s_ref[...] = a_col * s_prev + jnp.where(headmask, upd, 0.0)
    hs_out[0] = s_ref[...]
    o = o_scr[...]
    oo = o * o
    hi = oo.astype(BF16)
    lo = (oo - hi.astype(F32)).astype(BF16)
    ms = (jnp.dot(lo, ones_h, preferred_element_type=F32)
          + jnp.dot(hi, ones_h, preferred_element_type=F32)) * (1.0 / HGRN_DV)
    yd = o * lax.rsqrt(ms + EPS) * hog_ref[...] * og

    mp = jnp.zeros((tm, D_MODEL), F32)
    for n, y in ((0, ya), (1, yb), (3, yd)):
        gate = _sigmoid(jnp.dot(hb, wg_ref[:, n * D_MODEL:(n + 1) * D_MODEL], preferred_element_type=F32))
        mp = mp + gate * jnp.dot(y.astype(BF16), wbr_ref[n], preferred_element_type=F32)
    mp_out[0] = mp
    gc_out[0] = _sigmoid(jnp.dot(hb, wg_ref[:, 2 * D_MODEL:3 * D_MODEL], preferred_element_type=F32))


def _full(shape):
    n = len(shape)
    return pl.BlockSpec(shape, lambda *_: (0,) * n)


def _p1(x, sh1, sc1, lw, rope_tabs):
    bsz, t, d = x.shape
    tm = ROW_TILE
    assert t % tm == 0 and tm % GMLP_CHUNK == 0 and tm % LANES == 0
    bw = BRANCH_WIDTH
    row = lambda w: pl.BlockSpec((None, tm, w), lambda b, i: (b, i, 0))
    per_b = pl.BlockSpec((None, 1, d), lambda b, i: (b, 0, 0))
    tab = pl.BlockSpec((tm, LANES), lambda b, i: (i, 0))
    weights = [lw['n1g'], lw['wa'], lw['wkr'], lw['wh'], lw['whft'], lw['wg'], lw['wbr'],
               lw['gws'], lw['gbias'], lw['glng'], lw['glnb'],
               lw['cw'], lw['cb'], lw['clng'], lw['clnb'],
               lw['qng'], lw['wuqn'], lw['wuqr'], lw['wukbd'], lw['kvng']]
    tail = [lw['low'], lw['lowt'], lw['hog']]
    in_specs = ([pl.BlockSpec((1, tm, d), lambda b, i: (b, i, 0)), per_b, per_b]
                + [_full(w.shape) for w in weights] + [tab, tab, tab] + [_full(w.shape) for w in tail])
    out_shape = [jax.ShapeDtypeStruct((bsz, MLA_HEADS, t, 2 * LANES), BF16),
                 jax.ShapeDtypeStruct((bsz, t, 2 * LANES), BF16),
                 jax.ShapeDtypeStruct((bsz, t, MLA_KV_RANK), F32),
                 jax.ShapeDtypeStruct((bsz, t, MLA_ROPE), F32),
                 jax.ShapeDtypeStruct((bsz, t, d), F32),
                 jax.ShapeDtypeStruct((bsz, t, d), F32),
                 jax.ShapeDtypeStruct((bsz, CONV_PAD, bw), F32),
                 jax.ShapeDtypeStruct((bsz, bw, bw), F32)]
    out_specs = [pl.BlockSpec((1, MLA_HEADS, tm, 2 * LANES), lambda b, i: (b, 0, i, 0)),
                 pl.BlockSpec((1, tm, 2 * LANES), lambda b, i: (b, i, 0)),
                 pl.BlockSpec((1, tm, MLA_KV_RANK), lambda b, i: (b, i, 0)),
                 pl.BlockSpec((1, tm, MLA_ROPE), lambda b, i: (b, i, 0)),
                 pl.BlockSpec((1, tm, d), lambda b, i: (b, i, 0)),
                 pl.BlockSpec((1, tm, d), lambda b, i: (b, i, 0)),
                 pl.BlockSpec((1, CONV_PAD, bw), lambda b, i: (b, 0, 0)),
                 pl.BlockSpec((1, bw, bw), lambda b, i: (b, 0, 0))]
    return pl.pallas_call(
        _p1_kernel,
        grid=(bsz, t // tm),
        in_specs=in_specs,
        out_specs=out_specs,
        out_shape=out_shape,
        scratch_shapes=[pltpu.VMEM((CONV_PAD + tm, bw), F32),
                        pltpu.VMEM((bw, bw), F32),
                        pltpu.VMEM((tm, bw), F32)],
        compiler_params=pltpu.CompilerParams(dimension_semantics=("arbitrary", "arbitrary"),
                                             vmem_limit_bytes=VMEM_LIMIT),
        name="prompt_mixer_front",
    )(x, sh1, sc1, *weights, *rope_tabs, *tail)


def _attn_kernel(q_ref, k_ref, wuv_ref, o_ref, m_scr, l_scr, acc_scr):
    tq = q_ref.shape[2]
    i = pl.program_id(1)
    q = q_ref[0].reshape(MLA_HEADS * tq, 2 * LANES)
    m_scr[...] = jnp.full(m_scr.shape, MASK_VALUE, F32)
    l_scr[...] = jnp.zeros(l_scr.shape, F32)
    acc_scr[...] = jnp.zeros(acc_scr.shape, F32)

    def block(j, masked):
        kb = k_ref[0, pl.ds(pl.multiple_of(j * tq, tq), tq), :]
        s = _dot_nt(q, kb) * MLA_SCALE
        if masked:
            row = _iota(s.shape, 0) & (tq - 1)
            col = _iota(s.shape, 1)
            s = jnp.where(col <= row, s, MASK_VALUE)
        m_prev = m_scr[...]
        m_new = jnp.maximum(m_prev, jnp.max(s, axis=1, keepdims=True))
        alpha = jnp.exp(m_prev - m_new)
        p = jnp.exp(s - m_new)
        l_scr[...] = alpha * l_scr[...] + jnp.sum(p, axis=1, keepdims=True)
        acc_scr[...] = alpha * acc_scr[...] + jnp.dot(p.astype(BF16), kb[:, 0:LANES],
                                                      preferred_element_type=F32)
        m_scr[...] = m_new

    def body(j, carry):
        block(j, False)
        return carry

    lax.fori_loop(0, i, body, 0)
    block(i, True)
    o = acc_scr[...] / l_scr[...]
    yc = jnp.zeros((tq, BRANCH_WIDTH), F32)
    for h in range(MLA_HEADS):
        yc = yc + jnp.dot(o[h * tq:(h + 1) * tq].astype(BF16), wuv_ref[h], preferred_element_type=F32)
    o_ref[0] = yc


def _attn(qcat, kcat, wuv):
    bsz, nh, t, w = qcat.shape
    tq = ATTN_TILE
    assert t % tq == 0 and tq & (tq - 1) == 0
    return pl.pallas_call(
        _attn_kernel,
        grid=(bsz, t // tq),
        in_specs=[pl.BlockSpec((1, nh, tq, w), lambda b, i: (b, 0, i, 0)),
                  pl.BlockSpec((1, t, w), lambda b, i: (b, 0, 0)),
                  _full(wuv.shape)],
        out_specs=pl.BlockSpec((1, tq, BRANCH_WIDTH), lambda b, i: (b, i, 0)),
        out_shape=jax.ShapeDtypeStruct((bsz, t, BRANCH_WIDTH), F32),
        scratch_shapes=[pltpu.VMEM((nh * tq, 1), F32),
                        pltpu.VMEM((nh * tq, 1), F32),
                        pltpu.VMEM((nh * tq, LANES), F32)],
        compiler_params=pltpu.CompilerParams(dimension_semantics=("arbitrary", "arbitrary"),
                                             vmem_limit_bytes=VMEM_LIMIT),
        name="prompt_attention",
    )(qcat, kcat, wuv)


def _p3_kernel(*refs, n_late, final):
    x_ref, mp_ref = refs[0], refs[1]
    late = refs[2:2 + 3 * n_late]
    (gt1_ref, sh2_ref, sc2_ref, gt2_ref, wout_ref, n2g_ref, wup_ref, wdn_ref, fng_ref, o_ref) = refs[2 + 3 * n_late:]
    x = x_ref[...]
    m = mp_ref[...]
    for n in range(n_late):
        gate_ref, y_ref, w_ref = late[3 * n:3 * n + 3]
        m = m + gate_ref[...] * jnp.dot(y_ref[...].astype(BF16), w_ref[...], preferred_element_type=F32)
    x1 = x + gt1_ref[...] * jnp.dot(m.astype(BF16), wout_ref[...], preferred_element_type=F32)
    h2 = (_rms(x1, n2g_ref[...]) * (1.0 + sc2_ref[...]) + sh2_ref[...]).astype(BF16)
    acc = jnp.zeros(x.shape, F32)
    for c in range(MLP_HIDDEN // D_MODEL):
        up = jnp.dot(h2, wup_ref[:, c * D_MODEL:(c + 1) * D_MODEL], preferred_element_type=F32)
        up = jnp.square(jnp.maximum(up, 0.0))
        acc = acc + jnp.dot(up.astype(BF16), wdn_ref[c * D_MODEL:(c + 1) * D_MODEL, :],
                            preferred_element_type=F32)
    x2 = x1 + gt2_ref[...] * acc
    if final:
        x2 = _rms(x2, fng_ref[...])
    o_ref[...] = x2


def _p3(x, mp, late, mods, lw, fng, *, tm, rows_per_mod, final):
    rows, d = x.shape
    assert rows % tm == 0
    row = lambda w: pl.BlockSpec((tm, w), lambda i: (i, 0))
    if rows_per_mod is None:
        mod_spec = row(d)
    else:
        tiles_per_mod = rows_per_mod // tm
        mod_spec = pl.BlockSpec((None, 1, d), lambda i: (i // tiles_per_mod, 0, 0))
    args = [x, mp]
    in_specs = [row(d), row(d)]
    for gate, y, w in late:
        args += [gate, y, w]
        in_specs += [row(d), row(y.shape[1]), _full(w.shape)]
    weights = [lw['wout'], lw['n2g'], lw['wup'], lw['wdn'], fng]
    args += list(mods) + weights
    in_specs += [mod_spec] * 4 + [_full(w.shape) for w in weights]
    return pl.pallas_call(
        functools.partial(_p3_kernel, n_late=len(late), final=final),
        grid=(rows // tm,),
        in_specs=in_specs,
        out_specs=row(d),
        out_shape=jax.ShapeDtypeStruct((rows, d), F32),
        compiler_params=pltpu.CompilerParams(dimension_semantics=("arbitrary",),
                                             vmem_limit_bytes=VMEM_LIMIT),
        name="merge_channel_mixer",
    )(*args)


def _s1_kernel(x_ref, sh1_ref, sc1_ref, n1g_ref,
               wa_ref, wkr_ref, wh_ref, wg_ref, wbr_ref,
               gcw_ref, gcb_ref, glng_ref, glnb_ref,
               cw_ref, cb_ref, clng_ref, clnb_ref, cst_ref,
               qng_ref, wuqn_ref, wuqr_ref, wukbd_ref, kvng_ref,
               cos_ref, sina_ref, sinb_ref,
               low_ref, hog_ref, hst_ref, ex_ref, tl_ref,
               q_out, kn_out, lat_out, kpe_out, mp_out, gc_out, gd_out, yd_out, conv_out, hst_out, vrow_out):
    rt = x_ref.shape[0]
    bw = BRANCH_WIDTH
    x = x_ref[...]
    h = _rms(x, n1g_ref[...]) * (1.0 + sc1_ref[...]) + sh1_ref[...]
    hb = h.astype(BF16)
    za = jnp.dot(hb, wa_ref[...], preferred_element_type=F32)

    vn = _layernorm(_gelu(za[:, bw:2 * bw]), glng_ref[...], glnb_ref[...])
    ya = _gelu(za[:, 0:bw]) * (vn * gcw_ref[...] + gcb_ref[...])
    vrow_out[...] = vn

    g = za[:, 2 * bw:3 * bw] * _sigmoid(za[:, 3 * bw:4 * bw])
    acc = cw_ref[CONV_WIDTH - 1:CONV_WIDTH, :] * g
    for j in range(CONV_WIDTH - 1):
        acc = acc + cw_ref[j:j + 1, :] * cst_ref[j]
    yb = _silu(_layernorm(acc + cb_ref[...], clng_ref[...], clnb_ref[...]))
    for j in range(CONV_WIDTH - 2):
        conv_out[j] = cst_ref[j + 1]
    conv_out[CONV_WIDTH - 2] = g

    heads, lat, kpe = _mla_project(za, hb, wkr_ref, qng_ref, wuqn_ref, wuqr_ref, wukbd_ref, kvng_ref,
                                   cos_ref[...], sina_ref[...], sinb_ref[...])
    for hh, (ql, qp) in enumerate(heads):
        q_out[:, 2 * hh * LANES:(2 * hh + 1) * LANES] = ql
        q_out[:, (2 * hh + 1) * LANES:(2 * hh + 2) * LANES] = qp
    kn_out[:, 0:LANES] = lat
    kn_out[:, LANES:2 * LANES] = kpe
    lat_out[...] = lat
    kpe_out[...] = kpe[:, 0:MLA_ROPE]

    zh = jnp.dot(hb, wh_ref[...], preferred_element_type=F32)
    low = low_ref[...]
    hf = zh[:, bw:2 * bw]
    q = _silu(zh[:, 0:bw])
    decay = jnp.exp(jnp.log(jnp.maximum(low + (1.0 - low) * _sigmoid(hf), F_FLOOR)))
    kk = (1.0 - low) * _sigmoid(-hf)
    v = zh[:, 2 * bw:3 * bw]
    og = _silu(zh[:, 3 * bw:4 * bw])
    grp = _iota((rt, bw), 1) >> 6
    kv = HGRN_DK * HGRN_DV
    halves = []
    for hh in range(HGRN_HEADS):
        sel = grp == hh
        fx = _dot_sel_rhs(jnp.where(sel, decay, 0.0), ex_ref[...])
        kx = _dot_sel_rhs(jnp.where(sel, kk, 0.0), ex_ref[...])
        qx = _dot_sel_rhs(jnp.where(sel, q, 0.0), ex_ref[...])
        vx = _dot_sel_rhs(jnp.where(sel, v, 0.0), tl_ref[...])
        sn = fx * hst_ref[:, hh * kv:(hh + 1) * kv] + kx * vx
        hst_out[:, hh * kv:(hh + 1) * kv] = sn
        t = qx * sn
        o2 = t[:, 0:LANES]
        for c in range(1, kv // LANES):
            o2 = o2 + t[:, c * LANES:(c + 1) * LANES]
        halves.append(o2 + pltpu.roll(o2, HGRN_DV, 1))
    lane = _iota((rt, LANES), 1)
    o = jnp.concatenate([jnp.where(lane < HGRN_DV, halves[0], halves[1]),
                         jnp.where(lane < HGRN_DV, halves[2], halves[3])], axis=1)
    ones_h = jnp.where(_head_ones(), 1.0, 0.0).astype(BF16)
    ms = _dot_sel_rhs(o * o, ones_h) * (1.0 / HGRN_DV)
    yd_out[...] = o * lax.rsqrt(ms + EPS) * hog_ref[...] * og

    mp = jnp.zeros((rt, D_MODEL), F32)
    for n, y in ((0, ya), (1, yb)):
        gate = _sigmoid(jnp.dot(hb, wg_ref[:, n * D_MODEL:(n + 1) * D_MODEL], preferred_element_type=F32))
        mp = mp + gate * jnp.dot(y.astype(BF16), wbr_ref[n], preferred_element_type=F32)
    mp_out[...] = mp
    gc_out[...] = _sigmoid(jnp.dot(hb, wg_ref[:, 2 * D_MODEL:3 * D_MODEL], preferred_element_type=F32))
    gd_out[...] = _sigmoid(jnp.dot(hb, wg_ref[:, 3 * D_MODEL:4 * D_MODEL], preferred_element_type=F32))


def _s1(x, sh1, sc1, lw, rope_tabs, cst, hst, ex, tl):
    bs, d = x.shape
    rt = min(SAMPLE_TILE, bs)
    assert bs % rt == 0
    bw = BRANCH_WIDTH
    kvn = HGRN_HEADS * HGRN_DK * HGRN_DV
    row = lambda w: pl.BlockSpec((rt, w), lambda i: (i, 0))
    weights1 = [lw['n1g'], lw['wa'], lw['wkr'], lw['wh'], lw['wg'], lw['wbr'],
                lw['gcw'], lw['gcb'], lw['glng'], lw['glnb'],
                lw['cw'], lw['cb'], lw['clng'], lw['clnb']]
    weights2 = [lw['qng'], lw['wuqn'], lw['wuqr'], lw['wukbd'], lw['kvng']] + list(rope_tabs) + [lw['low'], lw['hog']]
    tm_spec = pl.BlockSpec((CONV_WIDTH - 1, rt, bw), lambda i: (0, i, 0))
    in_specs = ([row(d), row(d), row(d)] + [_full(w.shape) for w in weights1] + [tm_spec]
                + [_full(w.shape) for w in weights2] + [row(kvn), _full(ex.shape), _full(tl.shape)])
    out_shape = [jax.ShapeDtypeStruct((bs, MLA_HEADS * 2 * LANES), F32),
                 jax.ShapeDtypeStruct((bs, 2 * LANES), F32),
                 jax.ShapeDtypeStruct((bs, MLA_KV_RANK), F32),
                 jax.ShapeDtypeStruct((bs, MLA_ROPE), F32),
                 jax.ShapeDtypeStruct((bs, d), F32),
                 jax.ShapeDtypeStruct((bs, d), F32),
                 jax.ShapeDtypeStruct((bs, d), F32),
                 jax.ShapeDtypeStruct((bs, bw), F32),
                 jax.ShapeDtypeStruct((CONV_WIDTH - 1, bs, bw), F32),
                 jax.ShapeDtypeStruct((bs, kvn), F32),
                 jax.ShapeDtypeStruct((bs, bw), F32)]
    out_specs = [row(MLA_HEADS * 2 * LANES), row(2 * LANES),
                 row(MLA_KV_RANK), row(MLA_ROPE), row(d), row(d), row(d), row(bw),
                 tm_spec, row(kvn), row(bw)]
    return pl.pallas_call(
        _s1_kernel,
        grid=(bs // rt,),
        in_specs=in_specs,
        out_specs=out_specs,
        out_shape=out_shape,
        compiler_params=pltpu.CompilerParams(dimension_semantics=("arbitrary",),
                                             vmem_limit_bytes=VMEM_LIMIT),
        name="sample_mixer_front",
    )(x, sh1, sc1, *weights1, cst, *weights2, hst, ex, tl)


def _s2_kernel(pt_ref, q_ref, kn_ref, wuv_ref, clat_hbm, cpe_hbm, yc_out,
               latbuf, pebuf, sems, oall, *, layer, n_pages, page):
    b = pl.program_id(0)
    nb = pl.num_programs(0)
    slot = b % 2

    def copies(seq, sl):
        out = []
        for p in range(n_pages):
            pg = pt_ref[seq, p]
            out.append(pltpu.make_async_copy(clat_hbm.at[layer, pg], latbuf.at[sl, pl.ds(p * page, page), :],
                                             sems.at[0, sl]))
            out.append(pltpu.make_async_copy(cpe_hbm.at[layer, pg], pebuf.at[sl, pl.ds(p * page, page), :],
                                             sems.at[1, sl]))
        return out

    @pl.when(b == 0)
    def _():
        for c in copies(0, 0):
            c.start()

    @pl.when(b + 1 < nb)
    def _():
        for c in copies(b + 1, 1 - slot):
            c.start()

    for c in copies(b, slot):
        c.wait()

    latb = latbuf[slot].astype(BF16)
    peb = pebuf[slot].astype(BF16)
    qrow = q_ref[0]
    qrows = [qrow[:, 2 * h * LANES:(2 * h + 2) * LANES] for h in range(MLA_HEADS)]
    q = jnp.concatenate(qrows + [jnp.zeros((8 - MLA_HEADS, 2 * LANES), F32)], axis=0).astype(BF16)
    ql = q[:, 0:LANES]
    qp = q[:, LANES:LANES + MLA_ROPE]
    s = (_dot_nt(ql, latb) + _dot_nt(qp, peb)) * MLA_SCALE
    kn = kn_ref[0]
    latn = kn[:, 0:LANES].astype(BF16).astype(F32)
    pen = kn[:, LANES:2 * LANES].astype(BF16).astype(F32)
    s_new = (jnp.sum(ql.astype(F32) * latn, axis=1, keepdims=True)
             + jnp.sum(q[:, LANES:2 * LANES].astype(F32) * pen, axis=1, keepdims=True)) * MLA_SCALE
    m = jnp.maximum(jnp.max(s, axis=1, keepdims=True), s_new)
    p = jnp.exp(s - m)
    p_new = jnp.exp(s_new - m)
    l = jnp.sum(p, axis=1, keepdims=True) + p_new
    o = (jnp.dot(p.astype(BF16), latb, preferred_element_type=F32) + p_new * latn) / l
    for h in range(MLA_HEADS):
        oall[h, pl.ds(b, 1), :] = o[h:h + 1]

    @pl.when(b == nb - 1)
    def _():
        yc = jnp.zeros(yc_out.shape, F32)
        for h in range(MLA_HEADS):
            yc = yc + jnp.dot(oall[h].astype(BF16), wuv_ref[h], preferred_element_type=F32)
        yc_out[...] = yc


def _s2(page_table, q4, knew, wuv, cache_lat, cache_pe, layer):
    bs, n_pages = page_table.shape
    page = cache_lat.shape[2]
    past = n_pages * page
    grid_spec = pltpu.PrefetchScalarGridSpec(
        num_scalar_prefetch=1,
        grid=(bs,),
        in_specs=[pl.BlockSpec((1, 1, MLA_HEADS * 2 * LANES), lambda b, pt: (b, 0, 0)),
                  pl.BlockSpec((1, 1, 2 * LANES), lambda b, pt: (b, 0, 0)),
                  pl.BlockSpec(wuv.shape, lambda b, pt: (0, 0, 0)),
                  pl.BlockSpec(memory_space=pl.ANY),
                  pl.BlockSpec(memory_space=pl.ANY)],
        out_specs=pl.BlockSpec((bs, BRANCH_WIDTH), lambda b, pt: (0, 0)),
        scratch_shapes=[pltpu.VMEM((2, past, MLA_KV_RANK), F32),
                        pltpu.VMEM((2, past, MLA_ROPE), F32),
                        pltpu.SemaphoreType.DMA((2, 2)),
                        pltpu.VMEM((MLA_HEADS, bs, MLA_KV_RANK), F32)])
    return pl.pallas_call(
        functools.partial(_s2_kernel, layer=layer, n_pages=n_pages, page=page),
        grid_spec=grid_spec,
        out_shape=jax.ShapeDtypeStruct((bs, BRANCH_WIDTH), F32),
        compiler_params=pltpu.CompilerParams(dimension_semantics=("arbitrary",),
                                             vmem_limit_bytes=VMEM_LIMIT),
        name="sample_paged_attention",
    )(page_table, q4, knew, wuv, cache_lat, cache_pe)


def _rope_tables(pos):
    half = MLA_ROPE // 2
    inv = ROPE_THETA ** (-(2.0 / MLA_ROPE) * jnp.arange(half, dtype=F32))
    ang = pos[:, None] * inv[None, :]
    cos, sin = jnp.cos(ang), jnp.sin(ang)
    z = jnp.zeros((pos.shape[0], LANES - MLA_ROPE), F32)
    zh = jnp.zeros_like(cos)
    cos_t = jnp.concatenate([cos, cos, z], axis=1)
    sin_a = jnp.concatenate([-sin, zh, z], axis=1)
    sin_b = jnp.concatenate([zh, sin, z], axis=1)
    return cos_t, sin_a, sin_b


def _prep_layer(l, p, lower):
    bw = BRANCH_WIDTH
    w_in = p['w_in'][l]
    r1 = lambda a: a.reshape(1, -1)
    lw = {}
    lw['n1g'] = r1(p['norm1_g'][l])
    lw['n2g'] = r1(p['norm2_g'][l])
    lw['wa'] = w_in[:, :OFF_KR].astype(BF16)
    lw['wkr'] = jnp.pad(w_in[:, OFF_KR:OFF_H], ((0, 0), (0, LANES - MLA_ROPE))).astype(BF16)
    lw['wh'] = w_in[:, OFF_H:OFF_GATE].astype(BF16)
    lw['whft'] = w_in[:, OFF_HF:OFF_HF + bw].T.astype(BF16)
    lw['wg'] = w_in[:, OFF_GATE:].astype(BF16)
    lw['wbr'] = p['w_branch'][l].astype(BF16)
    lw['gws'] = p['gmlp_ws'][l]
    lw['gbias'] = jnp.repeat(p['gmlp_bs'][l].T, bw // GMLP_GROUPS, axis=1)
    lw['gcw'] = r1(jnp.repeat(p['gmlp_ws'][l][:, 0, 0], bw // GMLP_GROUPS))
    lw['gcb'] = r1(jnp.repeat(p['gmlp_bs'][l][:, 0], bw // GMLP_GROUPS))
    lw['glng'] = r1(p['gmlp_ln_g'][l])
    lw['glnb'] = r1(p['gmlp_ln_b'][l])
    lw['cw'] = jnp.pad(p['conv_w'][l], ((0, 32 - CONV_WIDTH), (0, 0)))
    lw['cb'] = r1(p['conv_b'][l])
    lw['clng'] = r1(p['conv_ln_g'][l])
    lw['clnb'] = r1(p['conv_ln_b'][l])
    lw['qng'] = r1(p['mla_q_norm_g'][l])
    uq = p['mla_w_uq'][l]
    lw['wuqn'] = uq[:, :, :MLA_NOPE].reshape(MLA_Q_RANK, MLA_HEADS * MLA_NOPE).astype(BF16)
    lw['wuqr'] = jnp.pad(uq[:, :, MLA_NOPE:], ((0, 0), (0, 0), (0, LANES - MLA_ROPE))).reshape(
        MLA_Q_RANK, MLA_HEADS * LANES).astype(BF16)
    uk = p['mla_w_uk'][l]
    wukbd = jnp.zeros((MLA_HEADS * MLA_NOPE, MLA_HEADS * MLA_KV_RANK), F32)
    uv = p['mla_w_uv'][l]
    wuv = jnp.zeros((MLA_HEADS, MLA_KV_RANK, bw), F32)
    for h in range(MLA_HEADS):
        wukbd = wukbd.at[h * MLA_NOPE:(h + 1) * MLA_NOPE, h * MLA_KV_RANK:(h + 1) * MLA_KV_RANK].set(uk[:, h, :].T)
        wuv = wuv.at[h, :, h * MLA_V:(h + 1) * MLA_V].set(uv[:, h, :])
    lw['wukbd'] = wukbd.astype(BF16)
    lw['wuv'] = wuv.astype(BF16)
    lw['kvng'] = r1(p['mla_kv_norm_g'][l])
    lw['low'] = r1(lower[l])
    lw['lowt'] = lower[l].reshape(-1, 1)
    lw['hog'] = r1(p['hgrn_out_norm_g'][l])
    lw['wout'] = p['w_out'][l].astype(BF16)
    lw['wup'] = p['mlp_w_up'][l].astype(BF16)
    lw['wdn'] = p['mlp_w_down'][l].astype(BF16)
    return lw


def kernel(x_prompt, x_sample, cache_kv_latent, cache_k_rope, state_conv, state_hgrn, page_table,
           c_prompt, c_sample, norm1_g, norm2_g, final_norm_g, w_ada, b_ada, w_in,
           gmlp_ln_g, gmlp_ln_b, gmlp_ws, gmlp_bs, conv_w, conv_b, conv_ln_g, conv_ln_b,
           mla_q_norm_g, mla_w_uq, mla_kv_norm_g, mla_w_uk, mla_w_uv,
           hgrn_lb_logits, hgrn_out_norm_g, w_branch, w_out, mlp_w_up, mlp_w_down):
    p = dict(norm1_g=norm1_g, norm2_g=norm2_g, w_in=w_in, gmlp_ln_g=gmlp_ln_g, gmlp_ln_b=gmlp_ln_b,
             gmlp_ws=gmlp_ws, gmlp_bs=gmlp_bs, conv_w=conv_w, conv_b=conv_b, conv_ln_g=conv_ln_g,
             conv_ln_b=conv_ln_b, mla_q_norm_g=mla_q_norm_g, mla_w_uq=mla_w_uq, mla_kv_norm_g=mla_kv_norm_g,
             mla_w_uk=mla_w_uk, mla_w_uv=mla_w_uv, hgrn_out_norm_g=hgrn_out_norm_g, w_branch=w_branch,
             w_out=w_out, mlp_w_up=mlp_w_up, mlp_w_down=mlp_w_down)
    depth = w_in.shape[0]
    bp, tp, d = x_prompt.shape
    bs = x_sample.shape[0]
    assert x_sample.shape[1] == 1
    n_pages = page_table.shape[1]
    page = cache_kv_latent.shape[2]
    past_len = n_pages * page
    bw = BRANCH_WIDTH

    sm = jax.nn.softmax(hgrn_lb_logits.astype(F32), axis=0)
    lower = jnp.cumsum(sm, axis=0) - sm[0]

    pad = (-bp) % 8
    c_all = jnp.concatenate([c_prompt, jnp.zeros((pad, d), F32), c_sample], axis=0)
    mod = _ada(c_all, w_ada, b_ada)
    mod_p = mod[:, :bp].reshape(depth, bp, ADA_PARTS, 1, d)
    mod_s = mod[:, bp + pad:].reshape(depth, bs, ADA_PARTS, d)

    tabs_p = _rope_tables(jnp.arange(tp, dtype=F32))
    tabs_s = _rope_tables(past_len + jnp.arange(1, dtype=F32))

    kidx = jnp.arange(HGRN_DK * HGRN_DV) // HGRN_DV
    vidx = jnp.arange(HGRN_DK * HGRN_DV) % HGRN_DV
    within = jnp.arange(bw) % HGRN_DK
    ex = (within[:, None] == kidx[None, :]).astype(BF16)
    tl = (within[:, None] == vidx[None, :]).astype(BF16)

    xp = x_prompt
    xs = x_sample.reshape(bs, d)
    cst_all = jnp.transpose(state_conv, (0, 2, 1, 3))
    hst_all = state_hgrn.reshape(depth, bs, HGRN_HEADS * HGRN_DK * HGRN_DV)

    lat_p, kpe_p, lat_s, kpe_s, conv_p, conv_s, hg_p, hg_s, v_s = [], [], [], [], [], [], [], [], []
    for l in range(depth):
        lw = _prep_layer(l, p, lower)
        final = l == depth - 1
        mp_l = [mod_p[l, :, k] for k in range(ADA_PARTS)]
        ms_l = [mod_s[l, :, k] for k in range(ADA_PARTS)]

        qcat, kcat, latp, kpep, mpart, gc, convp, hsp = _p1(xp, mp_l[0], mp_l[1], lw, tabs_p)
        yc = _attn(qcat, kcat, lw['wuv'])
        xp = _p3(xp.reshape(bp * tp, d), mpart.reshape(bp * tp, d),
                 [(gc.reshape(bp * tp, d), yc.reshape(bp * tp, bw), lw['wbr'][2])],
                 [mp_l[2], mp_l[3], mp_l[4], mp_l[5]], lw, final_norm_g.reshape(1, d),
                 tm=ROW_TILE, rows_per_mod=tp, final=final).reshape(bp, tp, d)
        lat_p.append(latp)
        kpe_p.append(kpep)
        conv_p.append(convp[:, CONV_PAD - (CONV_WIDTH - 1):])
        hg_p.append(jnp.stack([hsp[:, h * HGRN_DK:(h + 1) * HGRN_DK, h * HGRN_DV:(h + 1) * HGRN_DV]
                               for h in range(HGRN_HEADS)], axis=1))

        (q4, knew, lats, kpes, mps, gcs, gds, yds, convs, hsts, vrows) = _s1(
            xs, ms_l[0], ms_l[1], lw, tabs_s, cst_all[l], hst_all[l], ex, tl)
        ycs = _s2(page_table, q4.reshape(bs, 1, -1), knew.reshape(bs, 1, -1), lw['wuv'],
                  cache_kv_latent, cache_k_rope, l)
        xs = _p3(xs, mps, [(gcs, ycs, lw['wbr'][2]), (gds, yds, lw['wbr'][3])],
                 [ms_l[2], ms_l[3], ms_l[4], ms_l[5]], lw, final_norm_g.reshape(1, d),
                 tm=bs, rows_per_mod=None, final=final)
        lat_s.append(lats.reshape(bs, 1, MLA_KV_RANK))
        kpe_s.append(kpes.reshape(bs, 1, MLA_ROPE))
        conv_s.append(jnp.transpose(convs, (1, 0, 2)))
        hg_s.append(hsts.reshape(bs, HGRN_HEADS, HGRN_DK, HGRN_DV))
        v_s.append(vrows.reshape(bs, 1, bw))

    return (xp, xs.reshape(bs, 1, d),
            jnp.stack(lat_p), jnp.stack(kpe_p), jnp.stack(lat_s), jnp.stack(kpe_s),
            jnp.stack(conv_p), jnp.stack(conv_s), jnp.stack(hg_p), jnp.stack(hg_s), jnp.stack(v_s))
```
